```python
import jax, jax.numpy as jnp
from jax import lax
import numpy as np

D_MODEL = 4096
BATCH = 2
SEQ = 8192
DEPTH = 2

GRID_W = 64
CTX_LEN = 256
HEAD_DIM = 128
BRANCH_W = D_MODEL // 4
N_BRANCH = 4
A_HEADS = BRANCH_W // HEAD_DIM
A_KV_HEADS = max(1, A_HEADS // 4)
A_WINDOW = 128
A_BLOCK = 128
NA_HEADS = BRANCH_W // HEAD_DIM
NA_ROWS = 8
NA_COLS = 16
NA_COL_BLOCK = 16
POOL_GROUPS = 4
POOL_G = BRANCH_W // POOL_GROUPS
POOL_WINDOWS = (2, 4, 8, 16)
FNET_GROUPS = 4
D_FF = ((8 * D_MODEL // 3 + 255) // 256) * 256
ROPE_BASE = 10000.0
EPS = 1e-6

A_Q_W = A_HEADS * HEAD_DIM
A_KV_W = A_KV_HEADS * HEAD_DIM
NA_W = NA_HEADS * HEAD_DIM
OFF_AK = A_Q_W
OFF_AV = OFF_AK + A_KV_W
OFF_NQ = OFF_AV + A_KV_W
OFF_NK = OFF_NQ + NA_W
OFF_NV = OFF_NK + NA_W
OFF_PU = OFF_NV + NA_W
OFF_FU = OFF_PU + BRANCH_W
OFF_GT = OFF_FU + BRANCH_W
IN_TOTAL = OFF_GT + N_BRANCH * D_MODEL
IN_SPLITS = (OFF_AK, OFF_AV, OFF_NQ, OFF_NK, OFF_NV, OFF_PU, OFF_FU, OFF_GT)

kernel_name = 'hybrid_gated_window_natten_pool_fnet_dit'

F32 = jnp.float32


def rms_norm(x, g):
    xf = x.astype(F32)
    y = xf * lax.rsqrt(jnp.mean(xf * xf, axis=-1, keepdims=True) + EPS)
    return (y * g.astype(F32)).astype(x.dtype)


def heads(u, n):
    return u.reshape(u.shape[0], u.shape[1], n, HEAD_DIM)


def axial_rope(x, rows, cols):
    half = x.shape[-1] // 2
    inv = jnp.power(ROPE_BASE, -jnp.arange(0, half, 2, dtype=F32) / half)

    def rot(xp, pos):
        ang = pos.astype(F32)[:, None] * inv[None, :]
        cos = jnp.cos(ang)[None, :, None, :].astype(x.dtype)
        sin = jnp.sin(ang)[None, :, None, :].astype(x.dtype)
        x1, x2 = xp[..., : half // 2], xp[..., half // 2:]
        return jnp.concatenate([x1 * cos - x2 * sin, x1 * sin + x2 * cos], axis=-1)

    return jnp.concatenate([rot(x[..., :half], rows), rot(x[..., half:], cols)], axis=-1)


def window_attention(q, k, v, ck, cv, sink):
    Bn, S, Hq, dh = q.shape
    Hkv = k.shape[2]
    G = Hq // Hkv
    L = ck.shape[1]
    nb = S // A_BLOCK
    scale = dh ** -0.5
    qb = q.reshape(Bn, nb, A_BLOCK, Hkv, G, dh)

    def band(z):
        zp = jnp.pad(z, ((0, 0), (A_BLOCK, A_BLOCK), (0, 0), (0, 0))).reshape(Bn, nb + 2, A_BLOCK, Hkv, dh)
        return jnp.concatenate([zp[:, :-2], zp[:, 1:-1], zp[:, 2:]], axis=2)

    kw, vw = band(k), band(v)
    qi = jnp.arange(A_BLOCK)[:, None]
    kj = jnp.arange(3 * A_BLOCK)[None, :]
    kpos = jnp.arange(nb)[:, None] * A_BLOCK - A_BLOCK + kj
    mask = (jnp.abs(kj - A_BLOCK - qi) <= A_WINDOW)[None] & ((kpos >= 0) & (kpos < S))[:, None, :]
    s_loc = jnp.einsum('bnqhgd,bnkhd->bhgnqk', qb, kw).astype(F32) * scale
    s_loc = jnp.where(mask, s_loc, -jnp.inf)
    s_ctx = jnp.einsum('bnqhgd,blhd->bhgnql', qb, ck).astype(F32) * scale
    s_sink = jnp.broadcast_to(sink.astype(F32).reshape(1, Hkv, G, 1, 1, 1), s_ctx.shape[:-1] + (1,))
    p = jax.nn.softmax(jnp.concatenate([s_loc, s_ctx, s_sink], axis=-1), axis=-1).astype(v.dtype)
    nloc = 3 * A_BLOCK
    o = (jnp.einsum('bhgnqk,bnkhd->bnqhgd', p[..., :nloc], vw)
         + jnp.einsum('bhgnql,blhd->bnqhgd', p[..., nloc:nloc + L], cv))
    return o.reshape(Bn, S, Hq * dh)


def neighbourhood_attention(q, k, v, ck, cv, rpb):
    Bn, S, H, dh = q.shape
    rows = S // GRID_W
    kr = min(NA_ROWS, rows)
    ncb = GRID_W // NA_COL_BLOCK
    halo = NA_COL_BLOCK + NA_COLS
    r = jnp.arange(rows)
    row_idx = jnp.clip(r - kr // 2, 0, rows - kr)[:, None] + jnp.arange(kr)[None, :]
    jb = jnp.arange(ncb)
    col_idx = jnp.clip(jb * NA_COL_BLOCK - NA_COLS // 2, 0, GRID_W - halo)[:, None] + jnp.arange(halo)[None, :]
    qcol = jb[:, None] * NA_COL_BLOCK + jnp.arange(NA_COL_BLOCK)[None, :]
    cstart = jnp.clip(qcol - NA_COLS // 2, 0, GRID_W - NA_COLS)
    kc = col_idx[:, None, :]
    in_win = (kc >= cstart[..., None]) & (kc < cstart[..., None] + NA_COLS)
    dr = row_idx - r[:, None] + NA_ROWS - 1
    dc = jnp.clip(kc - qcol[..., None] + NA_COLS - 1, 0, 2 * NA_COLS - 2)
    bias = rpb[:, dr[:, None, None, :, None], dc[None, :, :, None, :]].astype(F32)
    bias = jnp.where(in_win[None, None, :, :, None, :], bias, -jnp.inf)
    qg = q.reshape(Bn, rows, ncb, NA_COL_BLOCK, H, dh)
    ridx = row_idx[:, None, :, None]
    cidx = col_idx[None, :, None, :]
    kn = k.reshape(Bn, rows, GRID_W, H, dh)[:, ridx, cidx]
    vn = v.reshape(Bn, rows, GRID_W, H, dh)[:, ridx, cidx]
    scale = dh ** -0.5
    nloc = kr * halo
    s_loc = jnp.einsum('brjqhd,brjkchd->bhrjqkc', qg, kn).astype(F32) * scale + bias[None]
    s_loc = s_loc.reshape(Bn, H, rows, ncb, NA_COL_BLOCK, nloc)
    s_ctx = jnp.einsum('brjqhd,blhd->bhrjql', qg, ck).astype(F32) * scale
    p = jax.nn.softmax(jnp.concatenate([s_loc, s_ctx], axis=-1), axis=-1).astype(v.dtype)
    p_loc = p[..., :nloc].reshape(Bn, H, rows, ncb, NA_COL_BLOCK, kr, halo)
    o = (jnp.einsum('bhrjqkc,brjkchd->brjqhd', p_loc, vn)
         + jnp.einsum('bhrjql,blhd->brjqhd', p[..., nloc:], cv))
    return o.reshape(Bn, S, H * dh)


def context_attention(q, k, v, sink):
    Bn, L, Hq, dh = q.shape
    Hkv = k.shape[2]
    G = Hq // Hkv
    qg = q.reshape(Bn, L, Hkv, G, dh)
    s = jnp.einsum('blhgd,bmhd->bhglm', qg, k).astype(F32) * (dh ** -0.5)
    if sink is None:
        p = jax.nn.softmax(s, axis=-1)
    else:
        s_sink = jnp.broadcast_to(sink.astype(F32).reshape(1, Hkv, G, 1, 1), s.shape[:-1] + (1,))
        p = jax.nn.softmax(jnp.concatenate([s, s_sink], axis=-1), axis=-1)[..., :L]
    o = jnp.einsum('bhglm,bmhd->blhgd', p.astype(v.dtype), v)
    return o.reshape(Bn, L, Hq * dh)


def multiscale_pool(u, w_pool, scale):
    Bn, N, C = u.shape
    ug = u.astype(F32).reshape(Bn, N, POOL_GROUPS, POOL_G)
    csum = jnp.concatenate([jnp.zeros_like(ug[:, :1]), jnp.cumsum(ug, axis=1)], axis=1)
    t = jnp.arange(N)[:, None]
    win = jnp.array(POOL_WINDOWS, dtype=jnp.int32)[None, :]
    lo = jnp.clip(t - win // 2, 0, N)
    hi = jnp.clip(t - win // 2 + win, 0, N)
    grp = jnp.arange(POOL_GROUPS)[None, :]
    mean = (csum[:, hi, grp] - csum[:, lo, grp]) / (hi - lo).astype(F32)[None, :, :, None]
    y = jnp.einsum('bngc,gcd->bngd', (mean - ug).astype(u.dtype), w_pool)
    return y.reshape(Bn, N, C) * scale


def fourier_mix(u, w):
    Bn, N, C = u.shape
    ug = u.astype(F32).reshape(Bn, N, FNET_GROUPS, C // FNET_GROUPS).transpose(0, 2, 1, 3)
    y = jnp.fft.fft2(ug, norm='ortho').real
    y = y.transpose(0, 2, 1, 3).reshape(Bn, N, C).astype(u.dtype)
    return y @ w


def gated_merge(ys, gate_logits, w_branch, w_out):
    merged = None
    for i in range(N_BRANCH):
        g = jax.nn.sigmoid(gate_logits[..., i * D_MODEL:(i + 1) * D_MODEL])
        term = g * (ys[i] @ w_branch[i])
        merged = term if merged is None else merged + term
    return merged @ w_out


def conv_ffn(h, w_gate, w_val, conv_w, conv_b, w_down):
    a = h @ w_gate
    zero = jnp.zeros_like(a[:, :1])
    a = (jnp.concatenate([zero, a[:, :-1]], axis=1) * conv_w[0] + a * conv_w[1]
         + jnp.concatenate([a[:, 1:], zero], axis=1) * conv_w[2] + conv_b)
    return (jax.nn.silu(a) * (h @ w_val)) @ w_down


def setup_inputs(seed: int = 0) -> dict:
    key = jax.random.key(seed)
    ks = jax.random.split(key, 24)
    nrm = jax.random.normal
    D = D_MODEL
    return {
        'x': nrm(ks[0], (BATCH, SEQ, D), F32),
        'c': nrm(ks[1], (BATCH, D), F32),
        'ctx': nrm(ks[2], (BATCH, CTX_LEN, D), F32),
        'c_ctx': nrm(ks[3], (D,), F32),
        'w_mod': nrm(ks[4], (DEPTH, D, 6 * D), F32) * (0.5 * D ** -0.5),
        'b_mod': nrm(ks[5], (DEPTH, 6 * D), F32) * 0.02,
        'g_mix': 1.0 + 0.05 * nrm(ks[6], (DEPTH, D), F32),
        'w_in': nrm(ks[7], (DEPTH, D, IN_TOTAL), F32) * D ** -0.5,
        'a_sink': nrm(ks[8], (DEPTH, A_HEADS), F32),
        'na_rpb': 0.5 * nrm(ks[9], (DEPTH, NA_HEADS, 2 * NA_ROWS - 1, 2 * NA_COLS - 1), F32),
        'w_pool': nrm(ks[10], (DEPTH, POOL_GROUPS, POOL_G, POOL_G), F32) * POOL_G ** -0.5,
        'pool_scale': 1.0 + 0.1 * nrm(ks[11], (DEPTH, BRANCH_W), F32),
        'w_fnet': nrm(ks[12], (DEPTH, BRANCH_W, BRANCH_W), F32) * BRANCH_W ** -0.5,
        'w_branch': nrm(ks[13], (DEPTH, N_BRANCH, BRANCH_W, D), F32) * BRANCH_W ** -0.5,
        'w_out': nrm(ks[14], (DEPTH, D, D), F32) * D ** -0.5,
        'g_ffn': 1.0 + 0.05 * nrm(ks[15], (DEPTH, D), F32),
        'w_ff_gate': nrm(ks[16], (DEPTH, D, D_FF), F32) * D ** -0.5,
        'w_ff_val': nrm(ks[17], (DEPTH, D, D_FF), F32) * D ** -0.5,
        'ff_conv_w': nrm(ks[18], (DEPTH, 3, D_FF), F32) * 3 ** -0.5,
        'ff_conv_b': 0.02 * nrm(ks[19], (DEPTH, D_FF), F32),
        'w_ff_down': nrm(ks[20], (DEPTH, D_FF, D), F32) * D_FF ** -0.5,
        'g_final': 1.0 + 0.05 * nrm(ks[21], (D,), F32),
    }


def reference(x, c, ctx, c_ctx, w_mod, b_mod, g_mix, w_in, a_sink, na_rpb, w_pool, pool_scale, w_fnet,
              w_branch, w_out, g_ffn, w_ff_gate, w_ff_val, ff_conv_w, ff_conv_b, w_ff_down, g_final):
    S = x.shape[1]
    t = jnp.arange(S)
    grid_row, grid_col = t // GRID_W, t % GRID_W
    cond = jax.nn.silu(c)
    cond_ctx = jax.nn.silu(c_ctx)
    xc = ctx
    for l in range(DEPTH):
        last = l == DEPTH - 1
        sh1, sc1, gt1, sh2, sc2, gt2 = jnp.split((cond @ w_mod[l] + b_mod[l])[:, None, :], 6, axis=-1)
        csh1, csc1, cgt1, csh2, csc2, cgt2 = jnp.split(cond_ctx @ w_mod[l] + b_mod[l], 6, axis=-1)
        h = rms_norm(x, g_mix[l]) * (1 + sc1) + sh1
        hc = rms_norm(xc, g_mix[l]) * (1 + csc1) + csh1
        aq, ak, av, nq, nk, nv, pool_u, fnet_u, gate_l = jnp.split(h @ w_in[l], IN_SPLITS, axis=-1)
        if last:
            cak, cav = jnp.split(hc @ w_in[l][:, OFF_AK:OFF_NQ], 2, axis=-1)
            cnk, cnv = jnp.split(hc @ w_in[l][:, OFF_NK:OFF_PU], 2, axis=-1)
        else:
            caq, cak, cav, cnq, cnk, cnv, cpool_u, cfnet_u, cgate_l = jnp.split(hc @ w_in[l], IN_SPLITS, axis=-1)
        ya = window_attention(axial_rope(heads(aq, A_HEADS), grid_row, grid_col),
                              axial_rope(heads(ak, A_KV_HEADS), grid_row, grid_col),
                              heads(av, A_KV_HEADS), heads(cak, A_KV_HEADS), heads(cav, A_KV_HEADS), a_sink[l])
        yn = neighbourhood_attention(heads(nq, NA_HEADS), heads(nk, NA_HEADS), heads(nv, NA_HEADS),
                                     heads(cnk, NA_HEADS), heads(cnv, NA_HEADS), na_rpb[l])
        yp = multiscale_pool(pool_u, w_pool[l], pool_scale[l])
        yf = fourier_mix(fnet_u, w_fnet[l])
        x = x + gt1 * gated_merge((ya, yn, yp, yf), gate_l, w_branch[l], w_out[l])
        h2 = rms_norm(x, g_ffn[l]) * (1 + sc2) + sh2
        x = x + gt2 * conv_ffn(h2, w_ff_gate[l], w_ff_val[l], ff_conv_w[l], ff_conv_b[l], w_ff_down[l])
        if not last:
            cya = context_attention(heads(caq, A_HEADS), heads(cak, A_KV_HEADS), heads(cav, A_KV_HEADS), a_sink[l])
            cyn = context_attention(heads(cnq, NA_HEADS), heads(cnk, NA_HEADS), heads(cnv, NA_HEADS), None)
            cyp = multiscale_pool(cpool_u, w_pool[l], pool_scale[l])
            cyf = fourier_mix(cfnet_u, w_fnet[l])
            xc = xc + cgt1 * gated_merge((cya, cyn, cyp, cyf), cgate_l, w_branch[l], w_out[l])
            hc2 = rms_norm(xc, g_ffn[l]) * (1 + csc2) + csh2
            xc = xc + cgt2 * conv_ffn(hc2, w_ff_gate[l], w_ff_val[l], ff_conv_w[l], ff_conv_b[l], w_ff_down[l])
    return rms_norm(x, g_final)
```

```python
import functools
import math

import jax
import jax.numpy as jnp
import numpy as np
from jax import lax
from jax.experimental import pallas as pl
from jax.experimental.pallas import tpu as pltpu

F32 = jnp.float32
BF16 = jnp.bfloat16
HI = lax.Precision.HIGHEST

GRID_W = 64
HEAD_DIM = 128
N_BRANCH = 4
A_WINDOW = 128
A_BLOCK = 128
NA_ROWS = 8
NA_COLS = 16
NA_ROW_GROUP = 8
NA_KEY_ROWS = 16
POOL_GROUPS = 4
POOL_WINDOWS = (2, 4, 8, 16)
POOL_HALO = 8
FNET_GROUPS = 4
ROPE_BASE = 10000.0
EPS = 1e-6
CONV_HALO = 16
MIB = 1 << 20


def _params(sem, vmem_mib):
    return pltpu.CompilerParams(dimension_semantics=sem, vmem_limit_bytes=vmem_mib * MIB)


def _sigmoid(v):
    return 1.0 / (1.0 + jnp.exp(-v))


def _mod_kernel(ct_ref, w_ref, b_ref, o_ref, acc_ref, *, n_rows):
    k = pl.program_id(2)

    @pl.when(k == 0)
    def _():
        acc_ref[...] = jnp.zeros_like(acc_ref)

    w = w_ref[0]
    ct = ct_ref[...]
    cs = ct * _sigmoid(ct)
    tk, tn = w.shape
    for r in range(n_rows):
        prod = cs[:, r:r + 1] * w
        acc_ref[r] += prod.reshape(tk // 8, 8, tn).sum(axis=0)

    @pl.when(k == pl.num_programs(2) - 1)
    def _():
        rid = lax.broadcasted_iota(jnp.int32, (8, tn), 0)
        out = jnp.zeros((8, tn), F32)
        for r in range(n_rows):
            out = jnp.where(rid == r, jnp.sum(acc_ref[r], axis=0, keepdims=True), out)
        o_ref[0] = out + b_ref[0]


def _modulation(cond_t, w_mod, b_mod, n_rows):
    depth, d, n6 = w_mod.shape
    tk = min(512, d)
    tn = min(2048, n6)
    return pl.pallas_call(
        functools.partial(_mod_kernel, n_rows=n_rows),
        out_shape=jax.ShapeDtypeStruct((depth, 8, n6), F32),
        grid=(depth, n6 // tn, d // tk),
        in_specs=[pl.BlockSpec((tk, 8), lambda l, j, k: (k, 0)),
                  pl.BlockSpec((1, tk, tn), lambda l, j, k: (l, k, j)),
                  pl.BlockSpec((1, 1, tn), lambda l, j, k: (l, 0, j))],
        out_specs=pl.BlockSpec((1, 8, tn), lambda l, j, k: (l, 0, j)),
        scratch_shapes=[pltpu.VMEM((n_rows, 8, tn), F32)],
        compiler_params=_params(("parallel", "parallel", "arbitrary"), 40),
        name="modulation",
    )(cond_t, w_mod, b_mod.reshape(depth, 1, n6))


def _norm_kernel(x_ref, g_ref, *rest, modulated):
    o_ref = rest[-1]
    x = x_ref[...]
    y = x * lax.rsqrt(jnp.mean(x * x, axis=-1, keepdims=True) + EPS)
    y = y * g_ref[...]
    if modulated:
        sc_ref, sh_ref = rest[0], rest[1]
        y = y * (1.0 + sc_ref[0]) + sh_ref[0]
    o_ref[...] = y.astype(o_ref.dtype)


def _norm(x, g, modv, row_of_tile, sc_idx, sh_idx, tm, out_dtype):
    m, d = x.shape
    modulated = modv is not None
    in_specs = [pl.BlockSpec((tm, d), lambda i: (i, 0)), pl.BlockSpec((1, d), lambda i: (0, 0))]
    args = [x, g.reshape(1, d)]
    if modulated:
        in_specs += [pl.BlockSpec((1, 1, d), lambda i: (row_of_tile(i) * 6 + sc_idx, 0, 0)),
                     pl.BlockSpec((1, 1, d), lambda i: (row_of_tile(i) * 6 + sh_idx, 0, 0))]
        args += [modv, modv]
    return pl.pallas_call(
        functools.partial(_norm_kernel, modulated=modulated),
        out_shape=jax.ShapeDtypeStruct((m, d), out_dtype),
        grid=(m // tm,),
        in_specs=in_specs,
        out_specs=pl.BlockSpec((tm, d), lambda i: (i, 0)),
        compiler_params=_params(("parallel",), 40),
        name="rmsnorm",
    )(*args)


def _swap32(v):
    n = v.shape[-1]
    lane = lax.broadcasted_iota(jnp.int32, v.shape, v.ndim - 1)
    fwd = pltpu.roll(v, n - 32, v.ndim - 1)
    bwd = pltpu.roll(v, 32, v.ndim - 1)
    return jnp.where((lane & 32) == 0, fwd, bwd)


def _inproj_kernel(h_ref, w_ref, cos_ref, sin_ref, qkv_ref, pu_ref, fu_ref, *, n_qkv, n_pu, n_rope_cols, rope):
    j = pl.program_id(1)
    acc = jnp.dot(h_ref[...], w_ref[...], preferred_element_type=F32)
    tn = acc.shape[1]

    if rope:
        @pl.when(j * tn < n_rope_cols)
        def _():
            reps = tn // HEAD_DIM
            cos = jnp.concatenate([cos_ref[...]] * reps, axis=1)
            sin = jnp.concatenate([sin_ref[...]] * reps, axis=1)
            col = j * tn + lax.broadcasted_iota(jnp.int32, acc.shape, 1)
            roped = acc * cos + _swap32(acc) * sin
            qkv_ref[...] = jnp.where(col < n_rope_cols, roped, acc).astype(qkv_ref.dtype)

        @pl.when((j * tn >= n_rope_cols) & (j < n_qkv))
        def _():
            qkv_ref[...] = acc.astype(qkv_ref.dtype)
    else:
        @pl.when(j < n_qkv)
        def _():
            qkv_ref[...] = acc.astype(qkv_ref.dtype)

    @pl.when((j >= n_qkv) & (j < n_qkv + n_pu))
    def _():
        pu_ref[...] = acc

    @pl.when(j >= n_qkv + n_pu)
    def _():
        fu_ref[...] = acc


def _inproj(h, w, cos_t, sin_t, dims, tm, rope, seq_len):
    m, d = h.shape
    bw = dims["bw"]
    tn = math.gcd(math.gcd(dims["off_pu"], bw), 512)
    n_qkv = dims["off_pu"] // tn
    n_pu = bw // tn
    n_tot = w.shape[1] // tn
    tiles_per_seq = seq_len // tm
    kern = functools.partial(_inproj_kernel, n_qkv=n_qkv, n_pu=n_pu,
                             n_rope_cols=dims["off_av"], rope=rope)
    return pl.pallas_call(
        kern,
        out_shape=(jax.ShapeDtypeStruct((m, dims["off_pu"]), BF16),
                   jax.ShapeDtypeStruct((m, bw), F32),
                   jax.ShapeDtypeStruct((m, bw), F32)),
        grid=(m // tm, n_tot),
        in_specs=[pl.BlockSpec((tm, d), lambda i, j: (i, 0)),
                  pl.BlockSpec((d, tn), lambda i, j: (0, j)),
                  pl.BlockSpec((tm, HEAD_DIM), lambda i, j: (i % tiles_per_seq, 0)),
                  pl.BlockSpec((tm, HEAD_DIM), lambda i, j: (i % tiles_per_seq, 0))],
        out_specs=(pl.BlockSpec((tm, tn), lambda i, j: (i, jnp.minimum(j, n_qkv - 1))),
                   pl.BlockSpec((tm, tn), lambda i, j: (i, jnp.clip(j - n_qkv, 0, n_pu - 1))),
                   pl.BlockSpec((tm, tn), lambda i, j: (i, jnp.clip(j - n_qkv - n_pu, 0, n_pu - 1)))),
        compiler_params=_params(("parallel", "arbitrary"), 48),
        name="inproj",
    )(h, w, cos_t, sin_t)


def _softmax_parts(parts, sink=None):
    m = parts[0].max(axis=-1, keepdims=True)
    for s in parts[1:]:
        m = jnp.maximum(m, s.max(axis=-1, keepdims=True))
    if sink is not None:
        m = jnp.maximum(m, sink)
    es = [jnp.exp(s - m) for s in parts]
    den = es[0].sum(axis=-1, keepdims=True)
    for e in es[1:]:
        den = den + e.sum(axis=-1, keepdims=True)
    if sink is not None:
        den = den + jnp.exp(sink - m)
    inv = 1.0 / den
    return [e * inv for e in es]


def _qk(q, k):
    return lax.dot_general(q, k, (((1,), (1,)), ((), ())), preferred_element_type=F32)


def _win_attn_kernel(sink_ref, q_ref, kp_ref, kc_ref, kn_ref, vp_ref, vc_ref, vn_ref, ck_ref, cv_ref, o_ref,
                     *, n_kv, group, scale):
    n = pl.program_id(1)
    nb = pl.num_programs(1)
    blk = q_ref.shape[0]
    rows = group * blk
    qi = lax.broadcasted_iota(jnp.int32, (rows, 3 * blk), 0) % blk
    kj = lax.broadcasted_iota(jnp.int32, (rows, 3 * blk), 1)
    kpos = (n - 1) * blk + kj
    valid = (jnp.abs(kj - blk - qi) <= A_WINDOW) & (kpos >= 0) & (kpos < nb * blk)
    for hk in range(n_kv):
        ks = slice(hk * HEAD_DIM, (hk + 1) * HEAD_DIM)
        k_loc = jnp.concatenate([kp_ref[:, ks], kc_ref[:, ks], kn_ref[:, ks]], axis=0)
        v_loc = jnp.concatenate([vp_ref[:, ks], vc_ref[:, ks], vn_ref[:, ks]], axis=0)
        heads = [hk * group + g for g in range(group)]
        q = jnp.concatenate([q_ref[:, h * HEAD_DIM:(h + 1) * HEAD_DIM] for h in heads], axis=0)
        sink = jnp.concatenate([jnp.full((blk, 1), sink_ref[h], F32) for h in heads], axis=0)
        s_loc = jnp.where(valid, _qk(q, k_loc) * scale, -jnp.inf)
        s_ctx = _qk(q, ck_ref[:, ks]) * scale
        p_loc, p_ctx = _softmax_parts([s_loc, s_ctx], sink)
        o = (jnp.dot(p_loc.astype(BF16), v_loc, preferred_element_type=F32)
             + jnp.dot(p_ctx.astype(BF16), cv_ref[:, ks], preferred_element_type=F32))
        for g, h in enumerate(heads):
            o_ref[:, h * HEAD_DIM:(h + 1) * HEAD_DIM] = o[g * blk:(g + 1) * blk].astype(o_ref.dtype)


def _win_attn(qkv, cqkv, sink, dims, batch, seq, ctx_len):
    blk = A_BLOCK
    nb = seq // blk
    n_q, n_kv = dims["a_heads"], dims["a_kv"]
    qw, kvw = n_q * HEAD_DIM, n_kv * HEAD_DIM
    kcol = dims["off_ak"] // kvw
    vcol = dims["off_av"] // kvw
    prev = lambda b, n: b * nb + jnp.maximum(n - 1, 0)
    cur = lambda b, n: b * nb + n
    nxt = lambda b, n: b * nb + jnp.minimum(n + 1, nb - 1)
    kv_spec = lambda rowf, col: pl.BlockSpec((blk, kvw), lambda b, n: (rowf(b, n), col))
    kern = functools.partial(_win_attn_kernel, n_kv=n_kv, group=n_q // n_kv, scale=HEAD_DIM ** -0.5)
    return pl.pallas_call(
        kern,
        out_shape=jax.ShapeDtypeStruct((batch * seq, qw), BF16),
        grid=(batch, nb),
        in_specs=[pl.BlockSpec(memory_space=pltpu.SMEM),
                  pl.BlockSpec((blk, qw), lambda b, n: (cur(b, n), 0)),
                  kv_spec(prev, kcol), kv_spec(cur, kcol), kv_spec(nxt, kcol),
                  kv_spec(prev, vcol), kv_spec(cur, vcol), kv_spec(nxt, vcol),
                  pl.BlockSpec((ctx_len, kvw), lambda b, n: (b, kcol)),
                  pl.BlockSpec((ctx_len, kvw), lambda b, n: (b, vcol))],
        out_specs=pl.BlockSpec((blk, qw), lambda b, n: (cur(b, n), 0)),
        compiler_params=_params(("parallel", "parallel"), 40),
        name="window_attention",
    )(sink, qkv, qkv, qkv, qkv, qkv, qkv, qkv, cqkv, cqkv)


def _ctx_attn_kernel(*refs, group, scale, use_sink):
    if use_sink:
        sink_ref, q_ref, k_ref, v_ref, o_ref = refs
    else:
        q_ref, k_ref, v_ref, o_ref = refs
    hk = pl.program_id(1)
    n = q_ref.shape[0]
    q = jnp.concatenate([q_ref[:, g * HEAD_DIM:(g + 1) * HEAD_DIM] for g in range(group)], axis=0)
    sink = None
    if use_sink:
        sink = jnp.concatenate([jnp.full((n, 1), sink_ref[hk * group + g], F32) for g in range(group)], axis=0)
    (p,) = _softmax_parts([_qk(q, k_ref[...]) * scale], sink)
    o = jnp.dot(p.astype(BF16), v_ref[...], preferred_element_type=F32)
    for g in range(group):
        o_ref[:, g * HEAD_DIM:(g + 1) * HEAD_DIM] = o[g * n:(g + 1) * n].astype(o_ref.dtype)


def _ctx_attn(cqkv, sink, batch, ctx_len, n_q, n_kv, q_off, k_off, v_off):
    group = n_q // n_kv
    gw = group * HEAD_DIM
    use_sink = sink is not None
    kern = functools.partial(_ctx_attn_kernel, group=group, scale=HEAD_DIM ** -0.5, use_sink=use_sink)
    in_specs = [pl.BlockSpec((ctx_len, gw), lambda b, hk: (b, q_off // gw + hk)),
                pl.BlockSpec((ctx_len, HEAD_DIM), lambda b, hk: (b, k_off // HEAD_DIM + hk)),
                pl.BlockSpec((ctx_len, HEAD_DIM), lambda b, hk: (b, v_off // HEAD_DIM + hk))]
    args = [cqkv, cqkv, cqkv]
    if use_sink:
        in_specs = [pl.BlockSpec(memory_space=pltpu.SMEM)] + in_specs
        args = [sink] + args
    return pl.pallas_call(
        kern,
        out_shape=jax.ShapeDtypeStruct((batch * ctx_len, n_q * HEAD_DIM), BF16),
        grid=(batch, n_kv),
        in_specs=in_specs,
        out_specs=pl.BlockSpec((ctx_len, gw), lambda b, hk: (b, hk)),
        compiler_params=_params(("parallel", "parallel"), 40),
        name="context_attention",
    )(*args)


def _nbr_tables(rows):
    kr = min(NA_ROWS, rows)
    n_rg = rows // NA_ROW_GROUP
    key_rows = min(NA_KEY_ROWS, rows)

    def key_start(rg):
        return int(np.clip(rg * NA_ROW_GROUP - kr // 2, 0, rows - key_rows))

    def tables(rg):
        r = rg * NA_ROW_GROUP + np.arange(NA_ROW_GROUP)[:, None, None, None]
        c = np.arange(GRID_W)[None, :, None, None]
        krow = key_start(rg) + np.arange(key_rows)[None, None, :, None]
        kc = np.arange(GRID_W)[None, None, None, :]
        rstart = np.clip(r - kr // 2, 0, rows - kr)
        cstart = np.clip(c - NA_COLS // 2, 0, GRID_W - NA_COLS)
        ok = (krow >= rstart) & (krow < rstart + kr) & (kc >= cstart) & (kc < cstart + NA_COLS)
        dr = np.clip(krow - r + NA_ROWS - 1, 0, 2 * NA_ROWS - 2)
        dc = np.clip(kc - c + NA_COLS - 1, 0, 2 * NA_COLS - 2)
        shape = (NA_ROW_GROUP * GRID_W, key_rows * GRID_W)
        full = np.broadcast_to
        return (full(dr, ok.shape).reshape(shape), full(dc, ok.shape).reshape(shape), ok.reshape(shape))

    if n_rg <= 3:
        variant_rgs = list(range(n_rg))
        variant_of = lambda rg: rg
    else:
        variant_rgs = [0, 1, n_rg - 1]
        variant_of = lambda rg: jnp.where(rg == 0, 0, jnp.where(rg == n_rg - 1, 2, 1))
    tabs = [tables(rg) for rg in variant_rgs]
    dr = np.stack([t[0] for t in tabs]).astype(np.int32)
    dc = np.stack([t[1] for t in tabs]).astype(np.int32)
    ok = np.stack([t[2] for t in tabs])
    starts = [key_start(rg) for rg in range(n_rg)]
    return dr, dc, ok, variant_of, starts, key_rows


def _nbr_attn_kernel(start_ref, q_ref, *refs, n_kblk, scale):
    k_refs = refs[:n_kblk]
    v_refs = refs[n_kblk:2 * n_kblk]
    ck_ref, cv_ref, bias_ref, o_ref = refs[2 * n_kblk:]
    q = q_ref[...]
    k_loc = jnp.concatenate([r[...] for r in k_refs], axis=0)
    v_loc = jnp.concatenate([r[...] for r in v_refs], axis=0)
    s_loc = _qk(q, k_loc) * scale + bias_ref[0, 0]
    s_ctx = _qk(q, ck_ref[...]) * scale
    p_loc, p_ctx = _softmax_parts([s_loc, s_ctx])
    o = (jnp.dot(p_loc.astype(BF16), v_loc, preferred_element_type=F32)
         + jnp.dot(p_ctx.astype(BF16), cv_ref[...], preferred_element_type=F32))
    o_ref[...] = o.astype(o_ref.dtype)


def _nbr_attn(qkv, cqkv, rpb, dims, batch, seq, ctx_len):
    rows = seq // GRID_W
    n_heads = dims["na_heads"]
    dr, dc, ok, variant_of, starts, key_rows = _nbr_tables(rows)
    bias = jnp.where(ok[None], rpb[:, dr, dc], -jnp.inf).astype(F32)
    n_rg = rows // NA_ROW_GROUP
    tq = NA_ROW_GROUP * GRID_W
    tk = key_rows * GRID_W
    kblk_rows = 4
    assert all(s % kblk_rows == 0 for s in starts) and key_rows % kblk_rows == 0
    kb = kblk_rows * GRID_W
    n_kblk = key_rows // kblk_rows
    seq_kb = seq // kb
    start_blk = jnp.asarray(np.array(starts, np.int32) // kblk_rows)
    qc, kc, vc = (dims[o] // HEAD_DIM for o in ("off_nq", "off_nk", "off_nv"))

    def kv_spec(col0, jb):
        return pl.BlockSpec((kb, HEAD_DIM), lambda b, rg, h, sb: (b * seq_kb + sb[rg] + jb, col0 + h))

    grid_spec = pltpu.PrefetchScalarGridSpec(
        num_scalar_prefetch=1,
        grid=(batch, n_rg, n_heads),
        in_specs=([pl.BlockSpec((tq, HEAD_DIM), lambda b, rg, h, sb: (b * n_rg + rg, qc + h))]
                  + [kv_spec(kc, jb) for jb in range(n_kblk)]
                  + [kv_spec(vc, jb) for jb in range(n_kblk)]
                  + [pl.BlockSpec((ctx_len, HEAD_DIM), lambda b, rg, h, sb: (b, kc + h)),
                     pl.BlockSpec((ctx_len, HEAD_DIM), lambda b, rg, h, sb: (b, vc + h)),
                     pl.BlockSpec((1, 1, tq, tk), lambda b, rg, h, sb: (h, variant_of(rg), 0, 0))]),
        out_specs=pl.BlockSpec((tq, HEAD_DIM), lambda b, rg, h, sb: (b * n_rg + rg, h)),
    )
    kern = functools.partial(_nbr_attn_kernel, n_kblk=n_kblk, scale=HEAD_DIM ** -0.5)
    return pl.pallas_call(
        kern,
        out_shape=jax.ShapeDtypeStruct((batch * seq, n_heads * HEAD_DIM), BF16),
        grid_spec=grid_spec,
        compiler_params=_params(("parallel", "parallel", "parallel"), 40),
        name="neighbourhood_attention",
    )(start_blk, qkv, *([qkv] * (2 * n_kblk)), cqkv, cqkv, bias)


def _pool_kernel(prev_ref, u_ref, next_ref, w_ref, s_ref, o_ref, buf_ref, *, seq_len):
    i = pl.program_id(1)
    n_i = pl.num_programs(1)
    tm, c = u_ref.shape
    g_w = c // POOL_GROUPS
    h = POOL_HALO
    buf_ref[0:h, :] = jnp.where(i > 0, prev_ref[...], 0.0)
    buf_ref[h:h + tm, :] = u_ref[...]
    buf_ref[h + tm:h + tm + h, :] = jnp.where(i < n_i - 1, next_ref[...], 0.0)
    t = i * tm + lax.broadcasted_iota(jnp.int32, (tm, 1), 0)
    for g, win in enumerate(POOL_WINDOWS):
        cols = slice(g * g_w, (g + 1) * g_w)
        half = win // 2
        tot = buf_ref[h - half:h - half + tm, cols]
        for d in range(-half + 1, half):
            tot = tot + buf_ref[h + d:h + d + tm, cols]
        cnt = jnp.minimum(t + half, seq_len) - jnp.maximum(t - half, 0)
        mean = tot / cnt.astype(F32)
        y = jnp.dot((mean - u_ref[:, cols]).astype(BF16), w_ref[g], preferred_element_type=F32)
        o_ref[:, cols] = (y * s_ref[:, cols]).astype(o_ref.dtype)


def _pool(u, w_pool, scale, n_seq, seq_len):
    m, c = u.shape
    tm = min(512, seq_len)
    n_i = seq_len // tm
    hb = tm // POOL_HALO
    last_hb = m // POOL_HALO - 1
    g_w = c // POOL_GROUPS
    return pl.pallas_call(
        functools.partial(_pool_kernel, seq_len=seq_len),
        out_shape=jax.ShapeDtypeStruct((m, c), BF16),
        grid=(n_seq, n_i),
        in_specs=[pl.BlockSpec((POOL_HALO, c), lambda s, i: (jnp.maximum((s * n_i + i) * hb - 1, 0), 0)),
                  pl.BlockSpec((tm, c), lambda s, i: (s * n_i + i, 0)),
                  pl.BlockSpec((POOL_HALO, c), lambda s, i: (jnp.minimum((s * n_i + i + 1) * hb, last_hb), 0)),
                  pl.BlockSpec((POOL_GROUPS, g_w, g_w), lambda s, i: (0, 0, 0)),
                  pl.BlockSpec((1, c), lambda s, i: (0, 0))],
        out_specs=pl.BlockSpec((tm, c), lambda s, i: (s * n_i + i, 0)),
        scratch_shapes=[pltpu.VMEM((tm + 2 * POOL_HALO, c), F32)],
        compiler_params=_params(("parallel", "parallel"), 40),
        name="multiscale_pool",
    )(u, u, u, w_pool, scale.reshape(1, c))


def _dft_mats(n):
    idx = np.arange(n)
    ang = 2.0 * np.pi * ((idx[:, None] * idx[None, :]) % n) / n
    return jnp.asarray(np.cos(ang), F32), jnp.asarray(np.sin(ang), F32)


def _fnet_split(n):
    n1 = 1 << (int(math.log2(n)) // 2)
    return n1, n // n1


def _fnet1_kernel(u_ref, c1_ref, s1_ref, twc_ref, tws_ref, zr_ref, zi_ref, *, tb, c):
    u = u_ref[...]
    zr = jnp.dot(c1_ref[...], u, precision=HI, preferred_element_type=F32)
    zi = -jnp.dot(s1_ref[...], u, precision=HI, preferred_element_type=F32)
    for tt in range(tb):
        cols = slice(tt * c, (tt + 1) * c)
        cw, sw = twc_ref[tt], tws_ref[tt]
        zr_ref[:, cols] = zr[:, cols] * cw + zi[:, cols] * sw
        zi_ref[:, cols] = zi[:, cols] * cw - zr[:, cols] * sw


def _fnet2_kernel(zr_ref, zi_ref, c2_ref, s2_ref, vr_ref, vi_ref, *, kb, c):
    c2, s2 = c2_ref[...], s2_ref[...]
    for kk in range(kb):
        cols = slice(kk * c, (kk + 1) * c)
        zr, zi = zr_ref[kk], zi_ref[kk]
        dot = lambda a, b: jnp.dot(a, b, precision=HI, preferred_element_type=F32)
        vr_ref[:, cols] = dot(c2, zr) + dot(s2, zi)
        vi_ref[:, cols] = dot(c2, zi) - dot(s2, zr)


def _fnet3_kernel(vr_ref, vi_ref, cc_ref, sc_ref, w_ref, o_ref, *, norm):
    c = vr_ref.shape[1]
    g_w = c // FNET_GROUPS
    dot = lambda a, b: jnp.dot(a, b, precision=HI, preferred_element_type=F32)
    ys = []
    for g in range(FNET_GROUPS):
        cols = slice(g * g_w, (g + 1) * g_w)
        ys.append((dot(vr_ref[:, cols], cc_ref[...]) + dot(vi_ref[:, cols], sc_ref[...])) * norm)
    y = jnp.concatenate(ys, axis=1).astype(BF16)
    o_ref[...] = jnp.dot(y, w_ref[...], preferred_element_type=F32).astype(o_ref.dtype)


def _fnet(u, w_fnet, n_seq, seq_len):
    m, c = u.shape
    g_w = c // FNET_GROUPS
    n1, n2 = _fnet_split(seq_len)
    c1, s1 = _dft_mats(n1)
    c2, s2 = _dft_mats(n2)
    cc, sc = _dft_mats(g_w)
    k1 = np.arange(n1)[None, :, None]
    t2 = np.arange(n2)[:, None, None]
    tw = 2.0 * np.pi * ((k1 * t2) % seq_len) / seq_len
    twc, tws = jnp.asarray(np.cos(tw), F32), jnp.asarray(np.sin(tw), F32)

    tb = min(4, n2)
    full = lambda shape: pl.BlockSpec(shape, lambda s, j: (0,) * len(shape))
    zr, zi = pl.pallas_call(
        functools.partial(_fnet1_kernel, tb=tb, c=c),
        out_shape=(jax.ShapeDtypeStruct((n_seq * n1, n2 * c), F32),) * 2,
        grid=(n_seq, n2 // tb),
        in_specs=[pl.BlockSpec((n1, tb * c), lambda s, j: (s, j)), full((n1, n1)), full((n1, n1)),
                  pl.BlockSpec((tb, n1, 1), lambda s, j: (j, 0, 0)),
                  pl.BlockSpec((tb, n1, 1), lambda s, j: (j, 0, 0))],
        out_specs=(pl.BlockSpec((n1, tb * c), lambda s, j: (s, j)),) * 2,
        compiler_params=_params(("parallel", "parallel"), 40),
        name="fnet_stage1",
    )(u.reshape(n_seq * n1, n2 * c), c1, s1, twc, tws)

    kb = min(4, n1)
    vr, vi = pl.pallas_call(
        functools.partial(_fnet2_kernel, kb=kb, c=c),
        out_shape=(jax.ShapeDtypeStruct((n_seq * n2, n1 * c), F32),) * 2,
        grid=(n_seq, n1 // kb),
        in_specs=[pl.BlockSpec((kb, n2, c), lambda s, j: (s * (n1 // kb) + j, 0, 0)),
                  pl.BlockSpec((kb, n2, c), lambda s, j: (s * (n1 // kb) + j, 0, 0)),
                  full((n2, n2)), full((n2, n2))],
        out_specs=(pl.BlockSpec((n2, kb * c), lambda s, j: (s, j)),) * 2,
        compiler_params=_params(("parallel", "parallel"), 40),
        name="fnet_stage2",
    )(zr.reshape(n_seq * n1, n2, c), zi.reshape(n_seq * n1, n2, c), c2, s2)

    tm = min(512, m)
    return pl.pallas_call(
        functools.partial(_fnet3_kernel, norm=float(1.0 / math.sqrt(seq_len * g_w))),
        out_shape=jax.ShapeDtypeStruct((m, c), BF16),
        grid=(m // tm,),
        in_specs=[pl.BlockSpec((tm, c), lambda i: (i, 0)), pl.BlockSpec((tm, c), lambda i: (i, 0)),
                  pl.BlockSpec((g_w, g_w), lambda i: (0, 0)), pl.BlockSpec((g_w, g_w), lambda i: (0, 0)),
                  pl.BlockSpec((c, c), lambda i: (0, 0))],
        out_specs=pl.BlockSpec((tm, c), lambda i: (i, 0)),
        compiler_params=_params(("parallel",), 40),
        name="fnet_stage3",
    )(vr.reshape(m, c), vi.reshape(m, c), cc, sc, w_fnet)


def _merge_kernel(h_ref, y0_ref, y1_ref, y2_ref, y3_ref, g0_ref, g1_ref, g2_ref, g3_ref, wb_ref, o_ref):
    h = h_ref[...]
    acc = None
    for i, (y_ref, g_ref) in enumerate(((y0_ref, g0_ref), (y1_ref, g1_ref), (y2_ref, g2_ref), (y3_ref, g3_ref))):
        gate = _sigmoid(jnp.dot(h, g_ref[...], preferred_element_type=F32))
        term = gate * jnp.dot(y_ref[...], wb_ref[i], preferred_element_type=F32)
        acc = term if acc is None else acc + term
    o_ref[...] = acc.astype(o_ref.dtype)


def _merge(h, ys, w_gate, w_branch, tm):
    m, d = h.shape
    bw = ys[0].shape[1]
    tn = 256
    n_j = d // tn
    gate_spec = lambda i: pl.BlockSpec((d, tn), lambda r, j: (0, i * n_j + j))
    return pl.pallas_call(
        _merge_kernel,
        out_shape=jax.ShapeDtypeStruct((m, d), BF16),
        grid=(m // tm, n_j),
        in_specs=([pl.BlockSpec((tm, d), lambda r, j: (r, 0))]
                  + [pl.BlockSpec((tm, bw), lambda r, j: (r, 0))] * N_BRANCH
                  + [gate_spec(i) for i in range(N_BRANCH)]
                  + [pl.BlockSpec((N_BRANCH, bw, tn), lambda r, j: (0, 0, j))]),
        out_specs=pl.BlockSpec((tm, tn), lambda r, j: (r, j)),
        compiler_params=_params(("parallel", "arbitrary"), 52),
        name="gated_merge",
    )(h, *ys, w_gate, w_gate, w_gate, w_gate, w_branch)


def _resproj_kernel(a_ref, w_ref, x_ref, gt_ref, o_ref):
    y = jnp.dot(a_ref[...], w_ref[...], preferred_element_type=F32)
    o_ref[...] = x_ref[...] + gt_ref[0] * y


def _resproj(a, w, x, modv, row_of_tile, gt_idx, tm, tn, vmem_mib):
    m, k = a.shape
    d = w.shape[1]
    return pl.pallas_call(
        _resproj_kernel,
        out_shape=jax.ShapeDtypeStruct((m, d), F32),
        grid=(m // tm, d // tn),
        in_specs=[pl.BlockSpec((tm, k), lambda i, j: (i, 0)),
                  pl.BlockSpec((k, tn), lambda i, j: (0, j)),
                  pl.BlockSpec((tm, tn), lambda i, j: (i, j)),
                  pl.BlockSpec((1, 1, tn), lambda i, j: (row_of_tile(i) * 6 + gt_idx, 0, j))],
        out_specs=pl.BlockSpec((tm, tn), lambda i, j: (i, j)),
        compiler_params=_params(("parallel", "arbitrary"), vmem_mib),
        name="residual_projection",
    )(a, w, x, modv)


def _ffn_kernel(prev_ref, h_ref, next_ref, wg_ref, wv_ref, cw_ref, cb_ref, o_ref, lhs_ref, *, tiles_per_seq):
    i = pl.program_id(0)
    j = pl.program_id(1)
    tm = h_ref.shape[0]
    hl = CONV_HALO

    @pl.when(j == 0)
    def _():
        lhs_ref[0:hl, :] = prev_ref[...]
        lhs_ref[hl:hl + tm, :] = h_ref[...]
        lhs_ref[hl + tm:hl + tm + hl, :] = next_ref[...]

    a = jnp.dot(lhs_ref[...], wg_ref[...], preferred_element_type=F32)
    ext = tm + 2 * hl
    row = lax.broadcasted_iota(jnp.int32, (tm, 1), 0)
    first = (i % tiles_per_seq) == 0
    last = (i % tiles_per_seq) == tiles_per_seq - 1
    a_prev = pltpu.roll(a, 1, 0)[hl:hl + tm]
    a_next = pltpu.roll(a, ext - 1, 0)[hl:hl + tm]
    a_prev = jnp.where((row == 0) & first, 0.0, a_prev)
    a_next = jnp.where((row == tm - 1) & last, 0.0, a_next)
    cw = cw_ref[...]
    conv = a_prev * cw[0:1] + a[hl:hl + tm] * cw[1:2] + a_next * cw[2:3] + cb_ref[...]
    val = jnp.dot(h_ref[...], wv_ref[...], preferred_element_type=F32)
    o_ref[...] = (conv * _sigmoid(conv) * val).astype(o_ref.dtype)


def _ffn(h, w_gate, w_val, conv_w, conv_b, tm, seq_len):
    m, d = h.shape
    f = w_gate.shape[1]
    tf = 256
    hb = tm // CONV_HALO
    last_hb = m // CONV_HALO - 1
    return pl.pallas_call(
        functools.partial(_ffn_kernel, tiles_per_seq=seq_len // tm),
        out_shape=jax.ShapeDtypeStruct((m, f), BF16),
        grid=(m // tm, f // tf),
        in_specs=[pl.BlockSpec((CONV_HALO, d), lambda i, j: (jnp.maximum(i * hb - 1, 0), 0)),
                  pl.BlockSpec((tm, d), lambda i, j: (i, 0)),
                  pl.BlockSpec((CONV_HALO, d), lambda i, j: (jnp.minimum((i + 1) * hb, last_hb), 0)),
                  pl.BlockSpec((d, tf), lambda i, j: (0, j)),
                  pl.BlockSpec((d, tf), lambda i, j: (0, j)),
                  pl.BlockSpec((3, tf), lambda i, j: (0, j)),
                  pl.BlockSpec((1, tf), lambda i, j: (0, j))],
        out_specs=pl.BlockSpec((tm, tf), lambda i, j: (i, j)),
        scratch_shapes=[pltpu.VMEM((tm + 2 * CONV_HALO, d), BF16)],
        compiler_params=_params(("parallel", "arbitrary"), 48),
        name="conv_glu",
    )(h, h, h, w_gate, w_val, conv_w, conv_b.reshape(1, f))


def _rope_tables(seq):
    half = HEAD_DIM // 2
    inv = jnp.power(ROPE_BASE, -jnp.arange(0, half, 2, dtype=F32) / half)
    t = jnp.arange(seq)

    def cs(pos):
        ang = pos.astype(F32)[:, None] * inv[None, :]
        return jnp.cos(ang), jnp.sin(ang)

    cr, sr = cs(t // GRID_W)
    cc, sc = cs(t % GRID_W)
    cos_t = jnp.concatenate([cr, cr, cc, cc], axis=1)
    sin_t = jnp.concatenate([-sr, sr, -sc, sc], axis=1)
    return cos_t, sin_t


def _dims(d):
    bw = d // N_BRANCH
    heads = bw // HEAD_DIM
    kv = max(1, heads // 4)
    off_ak = heads * HEAD_DIM
    off_av = off_ak + kv * HEAD_DIM
    off_nq = off_av + kv * HEAD_DIM
    off_nk = off_nq + bw
    off_nv = off_nk + bw
    off_pu = off_nv + bw
    off_fu = off_pu + bw
    off_gt = off_fu + bw
    return dict(bw=bw, a_heads=heads, a_kv=kv, na_heads=heads, off_ak=off_ak, off_av=off_av, off_nq=off_nq,
                off_nk=off_nk, off_nv=off_nv, off_pu=off_pu, off_fu=off_fu, off_gt=off_gt)


def kernel(x, c, ctx, c_ctx, w_mod, b_mod, g_mix, w_in, a_sink, na_rpb, w_pool, pool_scale, w_fnet, w_branch, w_out, g_ffn, w_ff_gate, w_ff_val, ff_conv_w, ff_conv_b, w_ff_down, g_final):
    batch, seq, d = x.shape
    ctx_len = ctx.shape[1]
    depth = w_mod.shape[0]
    dims = _dims(d)
    ctx_row = batch
    assert batch + 1 <= 8

    cond_t = jnp.zeros((d, 8), F32).at[:, :batch].set(c.T).at[:, ctx_row].set(c_ctx)
    mod = _modulation(cond_t, w_mod, b_mod, batch + 1)
    cos_t, sin_t = _rope_tables(seq)

    tm_lat = min(1024, seq)
    tm_ctx = min(256, ctx_len)
    lat_rows = lambda tm: (lambda i: i // (seq // tm))
    ctx_rows = lambda tm: (lambda i: ctx_row)

    xs = x.reshape(batch * seq, d)
    xc = ctx.reshape(batch * ctx_len, d)
    SH1, SC1, GT1, SH2, SC2, GT2 = range(6)

    for l in range(depth):
        last = l == depth - 1
        modv = mod[l].reshape(8 * 6, 1, d)
        w_qkv = w_in[l][:, :dims["off_gt"]].astype(BF16)
        w_gate = w_in[l][:, dims["off_gt"]:].astype(BF16)
        w_br = w_branch[l].astype(BF16)
        w_o = w_out[l].astype(BF16)
        w_pl = w_pool[l].astype(BF16)
        w_fn = w_fnet[l].astype(BF16)
        w_fg = w_ff_gate[l].astype(BF16)
        w_fv = w_ff_val[l].astype(BF16)
        w_fd = w_ff_down[l].astype(BF16)

        def mixer_inputs(xin, rows, tm, rope, slen):
            tmn = min(tm, 256)
            h = _norm(xin, g_mix[l], modv, rows(tmn), SC1, SH1, tmn, BF16)
            return (h,) + tuple(_inproj(h, w_qkv, cos_t, sin_t, dims, tm, rope, slen))

        def finish(xin, h, ys, rows, tm, slen):
            merged = _merge(h, ys, w_gate, w_br, min(tm, 512))
            x1 = _resproj(merged, w_o, xin, modv, rows(tm), GT1, tm, 512, 48)
            tmn = min(tm, 256)
            h2 = _norm(x1, g_ffn[l], modv, rows(tmn), SC2, SH2, tmn, BF16)
            gl = _ffn(h2, w_fg, w_fv, ff_conv_w[l], ff_conv_b[l], tm, slen)
            tmd = min(tm, 512)
            return _resproj(gl, w_fd, x1, modv, rows(tmd), GT2, tmd, 256, 52)

        hc, cqkv, cpu, cfu = mixer_inputs(xc, ctx_rows, tm_ctx, False, ctx_len)
        h, qkv, pu, fu = mixer_inputs(xs, lat_rows, tm_lat, True, seq)

        ya = _win_attn(qkv, cqkv, a_sink[l], dims, batch, seq, ctx_len)
        yn = _nbr_attn(qkv, cqkv, na_rpb[l], dims, batch, seq, ctx_len)
        yp = _pool(pu, w_pl, pool_scale[l], batch, seq)
        yf = _fnet(fu, w_fn, batch, seq)
        xs = finish(xs, h, (ya, yn, yp, yf), lat_rows, tm_lat, seq)

        if not last:
            cya = _ctx_attn(cqkv, a_sink[l], batch, ctx_len, dims["a_heads"], dims["a_kv"],
                            0, dims["off_ak"], dims["off_av"])
            cyn = _ctx_attn(cqkv, None, batch, ctx_len, dims["na_heads"], dims["na_heads"],
                            dims["off_nq"], dims["off_nk"], dims["off_nv"])
            cyp = _pool(cpu, w_pl, pool_scale[l], batch, ctx_len)
            cyf = _fnet(cfu, w_fn, batch, ctx_len)
            xc = finish(xc, hc, (cya, cyn, cyp, cyf), ctx_rows, tm_ctx, ctx_len)

    out = _norm(xs, g_final, None, None, None, None, 256, F32)
    return out.reshape(batch, seq, d)
```

```python
import functools
import math

import jax
import jax.numpy as jnp
import numpy as np
from jax import lax
from jax.experimental import pallas as pl
from jax.experimental.pallas import tpu as pltpu

F32 = jnp.float32
BF16 = jnp.bfloat16
HI = lax.Precision.HIGHEST

GRID_W = 64
HEAD_DIM = 128
N_BRANCH = 4
A_WINDOW = 128
A_BLOCK = 128
NA_ROWS = 8
NA_COLS = 16
NA_ROW_GROUP = 8
NA_KEY_ROWS = 16
POOL_GROUPS = 4
POOL_WINDOWS = (2, 4, 8, 16)
POOL_HALO = 8
FNET_GROUPS = 4
ROPE_BASE = 10000.0
EPS = 1e-6
CONV_HALO = 16
MIB = 1 << 20


def _params(sem, vmem_mib):
    return pltpu.CompilerParams(dimension_semantics=sem, vmem_limit_bytes=vmem_mib * MIB)


def _sigmoid(v):
    return 1.0 / (1.0 + jnp.exp(-v))


def _mod_kernel(ct_ref, w_ref, b_ref, o_ref, acc_ref, *, n_rows):
    k = pl.program_id(2)

    @pl.when(k == 0)
    def _():
        acc_ref[...] = jnp.zeros_like(acc_ref)

    w = w_ref[0]
    ct = ct_ref[...]
    cs = ct * _sigmoid(ct)
    tk, tn = w.shape
    for r in range(n_rows):
        prod = cs[:, r:r + 1] * w
        acc_ref[r] += prod.reshape(tk // 8, 8, tn).sum(axis=0)

    @pl.when(k == pl.num_programs(2) - 1)
    def _():
        rid = lax.broadcasted_iota(jnp.int32, (8, tn), 0)
        out = jnp.zeros((8, tn), F32)
        for r in range(n_rows):
            out = jnp.where(rid == r, jnp.sum(acc_ref[r], axis=0, keepdims=True), out)
        o_ref[0] = out + b_ref[0]


def _modulation(cond_t, w_mod, b_mod, n_rows):
    depth, d, n6 = w_mod.shape
    tk = min(512, d)
    tn = min(2048, n6)
    return pl.pallas_call(
        functools.partial(_mod_kernel, n_rows=n_rows),
        out_shape=jax.ShapeDtypeStruct((depth, 8, n6), F32),
        grid=(depth, n6 // tn, d // tk),
        in_specs=[pl.BlockSpec((tk, 8), lambda l, j, k: (k, 0)),
                  pl.BlockSpec((1, tk, tn), lambda l, j, k: (l, k, j)),
                  pl.BlockSpec((1, 1, tn), lambda l, j, k: (l, 0, j))],
        out_specs=pl.BlockSpec((1, 8, tn), lambda l, j, k: (l, 0, j)),
        scratch_shapes=[pltpu.VMEM((n_rows, 8, tn), F32)],
        compiler_params=_params(("parallel", "parallel", "arbitrary"), 40),
        name="modulation",
    )(cond_t, w_mod, b_mod.reshape(depth, 1, n6))


def _norm_kernel(x_ref, g_ref, *rest, modulated):
    o_ref = rest[-1]
    x = x_ref[...]
    y = x * lax.rsqrt(jnp.mean(x * x, axis=-1, keepdims=True) + EPS)
    y = y * g_ref[...]
    if modulated:
        sc_ref, sh_ref = rest[0], rest[1]
        y = y * (1.0 + sc_ref[0]) + sh_ref[0]
    o_ref[...] = y.astype(o_ref.dtype)


def _norm(x, g, modv, row_of_tile, sc_idx, sh_idx, tm, out_dtype):
    m, d = x.shape
    modulated = modv is not None
    in_specs = [pl.BlockSpec((tm, d), lambda i: (i, 0)), pl.BlockSpec((1, d), lambda i: (0, 0))]
    args = [x, g.reshape(1, d)]
    if modulated:
        in_specs += [pl.BlockSpec((1, 1, d), lambda i: (row_of_tile(i) * 6 + sc_idx, 0, 0)),
                     pl.BlockSpec((1, 1, d), lambda i: (row_of_tile(i) * 6 + sh_idx, 0, 0))]
        args += [modv, modv]
    return pl.pallas_call(
        functools.partial(_norm_kernel, modulated=modulated),
        out_shape=jax.ShapeDtypeStruct((m, d), out_dtype),
        grid=(m // tm,),
        in_specs=in_specs,
        out_specs=pl.BlockSpec((tm, d), lambda i: (i, 0)),
        compiler_params=_params(("parallel",), 40),
        name="rmsnorm",
    )(*args)


def _swap32(v):
    n = v.shape[-1]
    lane = lax.broadcasted_iota(jnp.int32, v.shape, v.ndim - 1)
    fwd = pltpu.roll(v, n - 32, v.ndim - 1)
    bwd = pltpu.roll(v, 32, v.ndim - 1)
    return jnp.where((lane & 32) == 0, fwd, bwd)


def _inproj_kernel(h_ref, w_ref, cos_ref, sin_ref, qkv_ref, pu_ref, fu_ref, *, n_qkv, n_pu, n_rope_cols, rope):
    j = pl.program_id(1)
    acc = jnp.dot(h_ref[...], w_ref[...], preferred_element_type=F32)
    tn = acc.shape[1]

    if rope:
        @pl.when(j * tn < n_rope_cols)
        def _():
            reps = tn // HEAD_DIM
            cos = jnp.concatenate([cos_ref[...]] * reps, axis=1)
            sin = jnp.concatenate([sin_ref[...]] * reps, axis=1)
            col = j * tn + lax.broadcasted_iota(jnp.int32, acc.shape, 1)
            roped = acc * cos + _swap32(acc) * sin
            qkv_ref[...] = jnp.where(col < n_rope_cols, roped, acc).astype(qkv_ref.dtype)

        @pl.when((j * tn >= n_rope_cols) & (j < n_qkv))
        def _():
            qkv_ref[...] = acc.astype(qkv_ref.dtype)
    else:
        @pl.when(j < n_qkv)
        def _():
            qkv_ref[...] = acc.astype(qkv_ref.dtype)

    @pl.when((j >= n_qkv) & (j < n_qkv + n_pu))
    def _():
        pu_ref[...] = acc

    @pl.when(j >= n_qkv + n_pu)
    def _():
        fu_ref[...] = acc


def _inproj(h, w, cos_t, sin_t, dims, tm, rope, seq_len):
    m, d = h.shape
    bw = dims["bw"]
    tn = math.gcd(math.gcd(dims["off_pu"], bw), 512)
    n_qkv = dims["off_pu"] // tn
    n_pu = bw // tn
    n_tot = w.shape[1] // tn
    tiles_per_seq = seq_len // tm
    kern = functools.partial(_inproj_kernel, n_qkv=n_qkv, n_pu=n_pu,
                             n_rope_cols=dims["off_av"], rope=rope)
    return pl.pallas_call(
        kern,
        out_shape=(jax.ShapeDtypeStruct((m, dims["off_pu"]), BF16),
                   jax.ShapeDtypeStruct((m, bw), F32),
                   jax.ShapeDtypeStruct((m, bw), F32)),
        grid=(m // tm, n_tot),
        in_specs=[pl.BlockSpec((tm, d), lambda i, j: (i, 0)),
                  pl.BlockSpec((d, tn), lambda i, j: (0, j)),
                  pl.BlockSpec((tm, HEAD_DIM), lambda i, j: (i % tiles_per_seq, 0)),
                  pl.BlockSpec((tm, HEAD_DIM), lambda i, j: (i % tiles_per_seq, 0))],
        out_specs=(pl.BlockSpec((tm, tn), lambda i, j: (i, jnp.minimum(j, n_qkv - 1))),
                   pl.BlockSpec((tm, tn), lambda i, j: (i, jnp.clip(j - n_qkv, 0, n_pu - 1))),
                   pl.BlockSpec((tm, tn), lambda i, j: (i, jnp.clip(j - n_qkv - n_pu, 0, n_pu - 1)))),
        compiler_params=_params(("parallel", "arbitrary"), 48),
        name="inproj",
    )(h, w, cos_t, sin_t)


def _softmax_parts(parts, sink=None):
    m = parts[0].max(axis=-1, keepdims=True)
    for s in parts[1:]:
        m = jnp.maximum(m, s.max(axis=-1, keepdims=True))
    if sink is not None:
        m = jnp.maximum(m, sink)
    es = [jnp.exp(s - m) for s in parts]
    den = es[0].sum(axis=-1, keepdims=True)
    for e in es[1:]:
        den = den + e.sum(axis=-1, keepdims=True)
    if sink is not None:
        den = den + jnp.exp(sink - m)
    inv = 1.0 / den
    return [e * inv for e in es]


def _qk(q, k):
    return lax.dot_general(q, k, (((1,), (1,)), ((), ())), preferred_element_type=F32)


def _win_attn_kernel(sink_ref, q_ref, kp_ref, kc_ref, kn_ref, vp_ref, vc_ref, vn_ref, ck_ref, cv_ref, o_ref,
                     *, n_kv, group, scale):
    n = pl.program_id(1)
    nb = pl.num_programs(1)
    blk = q_ref.shape[0]
    rows = group * blk
    qi = lax.broadcasted_iota(jnp.int32, (rows, 3 * blk), 0) % blk
    kj = lax.broadcasted_iota(jnp.int32, (rows, 3 * blk), 1)
    kpos = (n - 1) * blk + kj
    valid = (jnp.abs(kj - blk - qi) <= A_WINDOW) & (kpos >= 0) & (kpos < nb * blk)
    for hk in range(n_kv):
        ks = slice(hk * HEAD_DIM, (hk + 1) * HEAD_DIM)
        k_loc = jnp.concatenate([kp_ref[:, ks], kc_ref[:, ks], kn_ref[:, ks]], axis=0)
        v_loc = jnp.concatenate([vp_ref[:, ks], vc_ref[:, ks], vn_ref[:, ks]], axis=0)
        heads = [hk * group + g for g in range(group)]
        q = jnp.concatenate([q_ref[:, h * HEAD_DIM:(h + 1) * HEAD_DIM] for h in heads], axis=0)
        sink = jnp.concatenate([jnp.full((blk, 1), sink_ref[h], F32) for h in heads], axis=0)
        s_loc = jnp.where(valid, _qk(q, k_loc) * scale, -jnp.inf)
        s_ctx = _qk(q, ck_ref[:, ks]) * scale
        p_loc, p_ctx = _softmax_parts([s_loc, s_ctx], sink)
        o = (jnp.dot(p_loc.astype(BF16), v_loc, preferred_element_type=F32)
             + jnp.dot(p_ctx.astype(BF16), cv_ref[:, ks], preferred_element_type=F32))
        for g, h in enumerate(heads):
            o_ref[:, h * HEAD_DIM:(h + 1) * HEAD_DIM] = o[g * blk:(g + 1) * blk].astype(o_ref.dtype)


def _win_attn(qkv, cqkv, sink, dims, batch, seq, ctx_len):
    blk = A_BLOCK
    nb = seq // blk
    n_q, n_kv = dims["a_heads"], dims["a_kv"]
    qw, kvw = n_q * HEAD_DIM, n_kv * HEAD_DIM
    kcol = dims["off_ak"] // kvw
    vcol = dims["off_av"] // kvw
    prev = lambda b, n: b * nb + jnp.maximum(n - 1, 0)
    cur = lambda b, n: b * nb + n
    nxt = lambda b, n: b * nb + jnp.minimum(n + 1, nb - 1)
    kv_spec = lambda rowf, col: pl.BlockSpec((blk, kvw), lambda b, n: (rowf(b, n), col))
    kern = functools.partial(_win_attn_kernel, n_kv=n_kv, group=n_q // n_kv, scale=HEAD_DIM ** -0.5)
    return pl.pallas_call(
        kern,
        out_shape=jax.ShapeDtypeStruct((batch * seq, qw), BF16),
        grid=(batch, nb),
        in_specs=[pl.BlockSpec(memory_space=pltpu.SMEM),
                  pl.BlockSpec((blk, qw), lambda b, n: (cur(b, n), 0)),
                  kv_spec(prev, kcol), kv_spec(cur, kcol), kv_spec(nxt, kcol),
                  kv_spec(prev, vcol), kv_spec(cur, vcol), kv_spec(nxt, vcol),
                  pl.BlockSpec((ctx_len, kvw), lambda b, n: (b, kcol)),
                  pl.BlockSpec((ctx_len, kvw), lambda b, n: (b, vcol))],
        out_specs=pl.BlockSpec((blk, qw), lambda b, n: (cur(b, n), 0)),
        compiler_params=_params(("parallel", "parallel"), 40),
        name="window_attention",
    )(sink, qkv, qkv, qkv, qkv, qkv, qkv, qkv, cqkv, cqkv)


def _ctx_attn_kernel(*refs, group, scale, use_sink):
    if use_sink:
        sink_ref, q_ref, k_ref, v_ref, o_ref = refs
    else:
        q_ref, k_ref, v_ref, o_ref = refs
    hk = pl.program_id(1)
    n = q_ref.shape[0]
    q = jnp.concatenate([q_ref[:, g * HEAD_DIM:(g + 1) * HEAD_DIM] for g in range(group)], axis=0)
    sink = None
    if use_sink:
        sink = jnp.concatenate([jnp.full((n, 1), sink_ref[hk * group + g], F32) for g in range(group)], axis=0)
    (p,) = _softmax_parts([_qk(q, k_ref[...]) * scale], sink)
    o = jnp.dot(p.astype(BF16), v_ref[...], preferred_element_type=F32)
    for g in range(group):
        o_ref[:, g * HEAD_DIM:(g + 1) * HEAD_DIM] = o[g * n:(g + 1) * n].astype(o_ref.dtype)


def _ctx_attn(cqkv, sink, batch, ctx_len, n_q, n_kv, q_off, k_off, v_off):
    group = n_q // n_kv
    gw = group * HEAD_DIM
    use_sink = sink is not None
    kern = functools.partial(_ctx_attn_kernel, group=group, scale=HEAD_DIM ** -0.5, use_sink=use_sink)
    in_specs = [pl.BlockSpec((ctx_len, gw), lambda b, hk: (b, q_off // gw + hk)),
                pl.BlockSpec((ctx_len, HEAD_DIM), lambda b, hk: (b, k_off // HEAD_DIM + hk)),
                pl.BlockSpec((ctx_len, HEAD_DIM), lambda b, hk: (b, v_off // HEAD_DIM + hk))]
    args = [cqkv, cqkv, cqkv]
    if use_sink:
        in_specs = [pl.BlockSpec(memory_space=pltpu.SMEM)] + in_specs
        args = [sink] + args
    return pl.pallas_call(
        kern,
        out_shape=jax.ShapeDtypeStruct((batch * ctx_len, n_q * HEAD_DIM), BF16),
        grid=(batch, n_kv),
        in_specs=in_specs,
        out_specs=pl.BlockSpec((ctx_len, gw), lambda b, hk: (b, hk)),
        compiler_params=_params(("parallel", "parallel"), 40),
        name="context_attention",
    )(*args)


def _nbr_key_starts(rows):
    kr = min(NA_ROWS, rows)
    key_rows = min(NA_KEY_ROWS, rows)
    n_rg = rows // NA_ROW_GROUP
    return [int(np.clip(rg * NA_ROW_GROUP - kr // 2, 0, rows - key_rows)) for rg in range(n_rg)], key_rows


def _nbr_attn_kernel(start_ref, rpb_ref, q_ref, *refs, n_kblk, kblk_rows, rows, scale):
    k_refs = refs[:n_kblk]
    v_refs = refs[n_kblk:2 * n_kblk]
    ck_ref, cv_ref, o_ref, tl_ref, tr_ref = refs[2 * n_kblk:]
    h, b, rg = pl.program_id(0), pl.program_id(1), pl.program_id(2)
    n_dr, n_dc = 2 * NA_ROWS - 1, 2 * NA_COLS - 1
    neg = -jnp.inf

    @pl.when((b == 0) & (rg == 0))
    def _():
        c = lax.broadcasted_iota(jnp.int32, (GRID_W, 2 * GRID_W), 0)
        lane = lax.broadcasted_iota(jnp.int32, (GRID_W, 2 * GRID_W), 1)
        cstart = jnp.clip(c - NA_COLS // 2, 0, GRID_W - NA_COLS)
        for t_ref, kc in ((tl_ref, lane), (tr_ref, lane - GRID_W)):
            in_win = (kc >= cstart) & (kc < cstart + NA_COLS)
            dc_of = kc - c + NA_COLS - 1

            def body(dr, carry, t_ref=t_ref, in_win=in_win, dc_of=dc_of):
                acc = jnp.full(c.shape, neg, F32)
                for dc in range(n_dc):
                    acc = jnp.where(dc_of == dc, rpb_ref[(h * n_dr + dr) * n_dc + dc], acc)
                t_ref[dr] = jnp.where(in_win, acc, neg)
                return carry

            lax.fori_loop(0, n_dr, body, 0)
            t_ref[n_dr] = jnp.full(c.shape, neg, F32)

    kr = min(NA_ROWS, rows)
    key0 = start_ref[rg] * kblk_rows
    key_rows = n_kblk * kblk_rows

    def slot(r, rstart, krow):
        ok = (krow >= rstart) & (krow < rstart + kr)
        return jnp.where(ok, krow - r + NA_ROWS - 1, n_dr)

    bias_rows = []
    for rho in range(NA_ROW_GROUP):
        r = rg * NA_ROW_GROUP + rho
        rstart = jnp.clip(r - kr // 2, 0, rows - kr)
        tiles = [jnp.maximum(tl_ref[slot(r, rstart, key0 + kp)], tr_ref[slot(r, rstart, key0 + kp + 1)])
                 for kp in range(0, key_rows, 2)]
        bias_rows.append(jnp.concatenate(tiles, axis=1))
    bias = jnp.concatenate(bias_rows, axis=0)

    q = q_ref[...]
    k_loc = jnp.concatenate([r[...] for r in k_refs], axis=0)
    v_loc = jnp.concatenate([r[...] for r in v_refs], axis=0)
    s_loc = _qk(q, k_loc) * scale + bias
    s_ctx = _qk(q, ck_ref[...]) * scale
    p_loc, p_ctx = _softmax_parts([s_loc, s_ctx])
    o = (jnp.dot(p_loc.astype(BF16), v_loc, preferred_element_type=F32)
         + jnp.dot(p_ctx.astype(BF16), cv_ref[...], preferred_element_type=F32))
    o_ref[...] = o.astype(o_ref.dtype)


def _nbr_attn(qkv, cqkv, rpb, dims, batch, seq, ctx_len):
    rows = seq // GRID_W
    n_heads = dims["na_heads"]
    starts, key_rows = _nbr_key_starts(rows)
    n_rg = rows // NA_ROW_GROUP
    tq = NA_ROW_GROUP * GRID_W
    kblk_rows = 4
    assert all(s % kblk_rows == 0 for s in starts) and key_rows % kblk_rows == 0
    kb = kblk_rows * GRID_W
    n_kblk = key_rows // kblk_rows
    seq_kb = seq // kb
    start_blk = jnp.asarray(np.array(starts, np.int32) // kblk_rows)
    qc, kc, vc = (dims[o] // HEAD_DIM for o in ("off_nq", "off_nk", "off_nv"))

    def kv_spec(col0, jb):
        return pl.BlockSpec((kb, HEAD_DIM), lambda h, b, rg, sb: (b * seq_kb + sb[rg] + jb, col0 + h))

    grid_spec = pltpu.PrefetchScalarGridSpec(
        num_scalar_prefetch=1,
        grid=(n_heads, batch, n_rg),
        in_specs=([pl.BlockSpec(memory_space=pltpu.SMEM),
                   pl.BlockSpec((tq, HEAD_DIM), lambda h, b, rg, sb: (b * n_rg + rg, qc + h))]
                  + [kv_spec(kc, jb) for jb in range(n_kblk)]
                  + [kv_spec(vc, jb) for jb in range(n_kblk)]
                  + [pl.BlockSpec((ctx_len, HEAD_DIM), lambda h, b, rg, sb: (b, kc + h)),
                     pl.BlockSpec((ctx_len, HEAD_DIM), lambda h, b, rg, sb: (b, vc + h))]),
        out_specs=pl.BlockSpec((tq, HEAD_DIM), lambda h, b, rg, sb: (b * n_rg + rg, h)),
        scratch_shapes=[pltpu.VMEM((2 * NA_ROWS, GRID_W, 2 * GRID_W), F32)] * 2,
    )
    kern = functools.partial(_nbr_attn_kernel, n_kblk=n_kblk, kblk_rows=kblk_rows, rows=rows,
                             scale=HEAD_DIM ** -0.5)
    return pl.pallas_call(
        kern,
        out_shape=jax.ShapeDtypeStruct((batch * seq, n_heads * HEAD_DIM), BF16),
        grid_spec=grid_spec,
        compiler_params=_params(("arbitrary", "arbitrary", "arbitrary"), 40),
        name="neighbourhood_attention",
    )(start_blk, rpb.reshape(-1), qkv, *([qkv] * (2 * n_kblk)), cqkv, cqkv)


def _pool_kernel(prev_ref, u_ref, next_ref, w_ref, s_ref, o_ref, buf_ref, *, seq_len):
    i = pl.program_id(1)
    n_i = pl.num_programs(1)
    tm, c = u_ref.shape
    g_w = c // POOL_GROUPS
    h = POOL_HALO
    buf_ref[0:h, :] = jnp.where(i > 0, prev_ref[...], 0.0)
    buf_ref[h:h + tm, :] = u_ref[...]
    buf_ref[h + tm:h + tm + h, :] = jnp.where(i < n_i - 1, next_ref[...], 0.0)
    t = i * tm + lax.broadcasted_iota(jnp.int32, (tm, 1), 0)
    for g, win in enumerate(POOL_WINDOWS):
        cols = slice(g * g_w, (g + 1) * g_w)
        half = win // 2
        tot = buf_ref[h - half:h - half + tm, cols]
        for d in range(-half + 1, half):
            tot = tot + buf_ref[h + d:h + d + tm, cols]
        cnt = jnp.minimum(t + half, seq_len) - jnp.maximum(t - half, 0)
        mean = tot / cnt.astype(F32)
        y = jnp.dot((mean - u_ref[:, cols]).astype(BF16), w_ref[g], preferred_element_type=F32)
        o_ref[:, cols] = (y * s_ref[:, cols]).astype(o_ref.dtype)


def _pool(u, w_pool, scale, n_seq, seq_len):
    m, c = u.shape
    tm = min(512, seq_len)
    n_i = seq_len // tm
    hb = tm // POOL_HALO
    last_hb = m // POOL_HALO - 1
    g_w = c // POOL_GROUPS
    return pl.pallas_call(
        functools.partial(_pool_kernel, seq_len=seq_len),
        out_shape=jax.ShapeDtypeStruct((m, c), BF16),
        grid=(n_seq, n_i),
        in_specs=[pl.BlockSpec((POOL_HALO, c), lambda s, i: (jnp.maximum((s * n_i + i) * hb - 1, 0), 0)),
                  pl.BlockSpec((tm, c), lambda s, i: (s * n_i + i, 0)),
                  pl.BlockSpec((POOL_HALO, c), lambda s, i: (jnp.minimum((s * n_i + i + 1) * hb, last_hb), 0)),
                  pl.BlockSpec((POOL_GROUPS, g_w, g_w), lambda s, i: (0, 0, 0)),
                  pl.BlockSpec((1, c), lambda s, i: (0, 0))],
        out_specs=pl.BlockSpec((tm, c), lambda s, i: (s * n_i + i, 0)),
        scratch_shapes=[pltpu.VMEM((tm + 2 * POOL_HALO, c), F32)],
        compiler_params=_params(("parallel", "parallel"), 40),
        name="multiscale_pool",
    )(u, u, u, w_pool, scale.reshape(1, c))


def _dft_mats(n):
    idx = np.arange(n)
    ang = 2.0 * np.pi * ((idx[:, None] * idx[None, :]) % n) / n
    return jnp.asarray(np.cos(ang), F32), jnp.asarray(np.sin(ang), F32)


def _fnet_split(n):
    n1 = 1 << (int(math.log2(n)) // 2)
    return n1, n // n1


def _fnet1_kernel(u_ref, c1_ref, s1_ref, twc_ref, tws_ref, zr_ref, zi_ref, *, tb, c):
    u = u_ref[...]
    zr = jnp.dot(c1_ref[...], u, precision=HI, preferred_element_type=F32)
    zi = -jnp.dot(s1_ref[...], u, precision=HI, preferred_element_type=F32)
    for tt in range(tb):
        cols = slice(tt * c, (tt + 1) * c)
        cw, sw = twc_ref[tt], tws_ref[tt]
        zr_ref[:, cols] = zr[:, cols] * cw + zi[:, cols] * sw
        zi_ref[:, cols] = zi[:, cols] * cw - zr[:, cols] * sw


def _fnet2_kernel(zr_ref, zi_ref, c2_ref, s2_ref, vr_ref, vi_ref, *, kb, c):
    c2, s2 = c2_ref[...], s2_ref[...]
    for kk in range(kb):
        cols = slice(kk * c, (kk + 1) * c)
        zr, zi = zr_ref[kk], zi_ref[kk]
        dot = lambda a, b: jnp.dot(a, b, precision=HI, preferred_element_type=F32)
        vr_ref[:, cols] = dot(c2, zr) + dot(s2, zi)
        vi_ref[:, cols] = dot(c2, zi) - dot(s2, zr)


def _fnet3_kernel(vr_ref, vi_ref, cc_ref, sc_ref, w_ref, o_ref, *, norm):
    c = vr_ref.shape[1]
    g_w = c // FNET_GROUPS
    dot = lambda a, b: jnp.dot(a, b, precision=HI, preferred_element_type=F32)
    ys = []
    for g in range(FNET_GROUPS):
        cols = slice(g * g_w, (g + 1) * g_w)
        ys.append((dot(vr_ref[:, cols], cc_ref[...]) + dot(vi_ref[:, cols], sc_ref[...])) * norm)
    y = jnp.concatenate(ys, axis=1).astype(BF16)
    o_ref[...] = jnp.dot(y, w_ref[...], preferred_element_type=F32).astype(o_ref.dtype)


def _fnet(u, w_fnet, n_seq, seq_len):
    m, c = u.shape
    g_w = c // FNET_GROUPS
    n1, n2 = _fnet_split(seq_len)
    c1, s1 = _dft_mats(n1)
    c2, s2 = _dft_mats(n2)
    cc, sc = _dft_mats(g_w)
    k1 = np.arange(n1)[None, :, None]
    t2 = np.arange(n2)[:, None, None]
    tw = 2.0 * np.pi * ((k1 * t2) % seq_len) / seq_len
    twc, tws = jnp.asarray(np.cos(tw), F32), jnp.asarray(np.sin(tw), F32)

    tb = min(4, n2)
    full = lambda shape: pl.BlockSpec(shape, lambda s, j: (0,) * len(shape))
    zr, zi = pl.pallas_call(
        functools.partial(_fnet1_kernel, tb=tb, c=c),
        out_shape=(jax.ShapeDtypeStruct((n_seq * n1, n2 * c), F32),) * 2,
        grid=(n_seq, n2 // tb),
        in_specs=[pl.BlockSpec((n1, tb * c), lambda s, j: (s, j)), full((n1, n1)), full((n1, n1)),
                  pl.BlockSpec((tb, n1, 1), lambda s, j: (j, 0, 0)),
                  pl.BlockSpec((tb, n1, 1), lambda s, j: (j, 0, 0))],
        out_specs=(pl.BlockSpec((n1, tb * c), lambda s, j: (s, j)),) * 2,
        compiler_params=_params(("parallel", "parallel"), 40),
        name="fnet_stage1",
    )(u.reshape(n_seq * n1, n2 * c), c1, s1, twc, tws)

    kb = min(4, n1)
    vr, vi = pl.pallas_call(
        functools.partial(_fnet2_kernel, kb=kb, c=c),
        out_shape=(jax.ShapeDtypeStruct((n_seq * n2, n1 * c), F32),) * 2,
        grid=(n_seq, n1 // kb),
        in_specs=[pl.BlockSpec((kb, n2, c), lambda s, j: (s * (n1 // kb) + j, 0, 0)),
                  pl.BlockSpec((kb, n2, c), lambda s, j: (s * (n1 // kb) + j, 0, 0)),
                  full((n2, n2)), full((n2, n2))],
        out_specs=(pl.BlockSpec((n2, kb * c), lambda s, j: (s, j)),) * 2,
        compiler_params=_params(("parallel", "parallel"), 40),
        name="fnet_stage2",
    )(zr.reshape(n_seq * n1, n2, c), zi.reshape(n_seq * n1, n2, c), c2, s2)

    tm = min(512, m)
    return pl.pallas_call(
        functools.partial(_fnet3_kernel, norm=float(1.0 / math.sqrt(seq_len * g_w))),
        out_shape=jax.ShapeDtypeStruct((m, c), BF16),
        grid=(m // tm,),
        in_specs=[pl.BlockSpec((tm, c), lambda i: (i, 0)), pl.BlockSpec((tm, c), lambda i: (i, 0)),
                  pl.BlockSpec((g_w, g_w), lambda i: (0, 0)), pl.BlockSpec((g_w, g_w), lambda i: (0, 0)),
                  pl.BlockSpec((c, c), lambda i: (0, 0))],
        out_specs=pl.BlockSpec((tm, c), lambda i: (i, 0)),
        compiler_params=_params(("parallel",), 40),
        name="fnet_stage3",
    )(vr.reshape(m, c), vi.reshape(m, c), cc, sc, w_fnet)


def _merge_kernel(h_ref, y0_ref, y1_ref, y2_ref, y3_ref, g0_ref, g1_ref, g2_ref, g3_ref, wb_ref, o_ref):
    h = h_ref[...]
    acc = None
    for i, (y_ref, g_ref) in enumerate(((y0_ref, g0_ref), (y1_ref, g1_ref), (y2_ref, g2_ref), (y3_ref, g3_ref))):
        gate = _sigmoid(jnp.dot(h, g_ref[...], preferred_element_type=F32))
        term = gate * jnp.dot(y_ref[...], wb_ref[i], preferred_element_type=F32)
        acc = term if acc is None else acc + term
    o_ref[...] = acc.astype(o_ref.dtype)


def _merge(h, ys, w_gate, w_branch, tm):
    m, d = h.shape
    bw = ys[0].shape[1]
    tn = 256
    n_j = d // tn
    gate_spec = lambda i: pl.BlockSpec((d, tn), lambda r, j: (0, i * n_j + j))
    return pl.pallas_call(
        _merge_kernel,
        out_shape=jax.ShapeDtypeStruct((m, d), BF16),
        grid=(m // tm, n_j),
        in_specs=([pl.BlockSpec((tm, d), lambda r, j: (r, 0))]
                  + [pl.BlockSpec((tm, bw), lambda r, j: (r, 0))] * N_BRANCH
                  + [gate_spec(i) for i in range(N_BRANCH)]
                  + [pl.BlockSpec((N_BRANCH, bw, tn), lambda r, j: (0, 0, j))]),
        out_specs=pl.BlockSpec((tm, tn), lambda r, j: (r, j)),
        compiler_params=_params(("parallel", "arbitrary"), 52),
        name="gated_merge",
    )(h, *ys, w_gate, w_gate, w_gate, w_gate, w_branch)


def _resproj_kernel(a_ref, w_ref, x_ref, gt_ref, o_ref):
    y = jnp.dot(a_ref[...], w_ref[...], preferred_element_type=F32)
    o_ref[...] = x_ref[...] + gt_ref[0] * y


def _resproj(a, w, x, modv, row_of_tile, gt_idx, tm, tn, vmem_mib):
    m, k = a.shape
    d = w.shape[1]
    return pl.pallas_call(
        _resproj_kernel,
        out_shape=jax.ShapeDtypeStruct((m, d), F32),
        grid=(m // tm, d // tn),
        in_specs=[pl.BlockSpec((tm, k), lambda i, j: (i, 0)),
                  pl.BlockSpec((k, tn), lambda i, j: (0, j)),
                  pl.BlockSpec((tm, tn), lambda i, j: (i, j)),
                  pl.BlockSpec((1, 1, tn), lambda i, j: (row_of_tile(i) * 6 + gt_idx, 0, j))],
        out_specs=pl.BlockSpec((tm, tn), lambda i, j: (i, j)),
        compiler_params=_params(("parallel", "arbitrary"), vmem_mib),
        name="residual_projection",
    )(a, w, x, modv)


def _ffn_kernel(prev_ref, h_ref, next_ref, wg_ref, wv_ref, cw_ref, cb_ref, o_ref, lhs_ref, *, tiles_per_seq):
    i = pl.program_id(0)
    j = pl.program_id(1)
    tm = h_ref.shape[0]
    hl = CONV_HALO

    @pl.when(j == 0)
    def _():
        lhs_ref[0:hl, :] = prev_ref[...]
        lhs_ref[hl:hl + tm, :] = h_ref[...]
        lhs_ref[hl + tm:hl + tm + hl, :] = next_ref[...]

    a = jnp.dot(lhs_ref[...], wg_ref[...], preferred_element_type=F32)
    ext = tm + 2 * hl
    row = lax.broadcasted_iota(jnp.int32, (tm, 1), 0)
    first = (i % tiles_per_seq) == 0
    last = (i % tiles_per_seq) == tiles_per_seq - 1
    a_prev = pltpu.roll(a, 1, 0)[hl:hl + tm]
    a_next = pltpu.roll(a, ext - 1, 0)[hl:hl + tm]
    a_prev = jnp.where((row == 0) & first, 0.0, a_prev)
    a_next = jnp.where((row == tm - 1) & last, 0.0, a_next)
    cw = cw_ref[...]
    conv = a_prev * cw[0:1] + a[hl:hl + tm] * cw[1:2] + a_next * cw[2:3] + cb_ref[...]
    val = jnp.dot(h_ref[...], wv_ref[...], preferred_element_type=F32)
    o_ref[...] = (conv * _sigmoid(conv) * val).astype(o_ref.dtype)


def _ffn(h, w_gate, w_val, conv_w, conv_b, tm, seq_len):
    m, d = h.shape
    f = w_gate.shape[1]
    tf = 256
    hb = tm // CONV_HALO
    last_hb = m // CONV_HALO - 1
    return pl.pallas_call(
        functools.partial(_ffn_kernel, tiles_per_seq=seq_len // tm),
        out_shape=jax.ShapeDtypeStruct((m, f), BF16),
        grid=(m // tm, f // tf),
        in_specs=[pl.BlockSpec((CONV_HALO, d), lambda i, j: (jnp.maximum(i * hb - 1, 0), 0)),
                  pl.BlockSpec((tm, d), lambda i, j: (i, 0)),
                  pl.BlockSpec((CONV_HALO, d), lambda i, j: (jnp.minimum((i + 1) * hb, last_hb), 0)),
                  pl.BlockSpec((d, tf), lambda i, j: (0, j)),
                  pl.BlockSpec((d, tf), lambda i, j: (0, j)),
                  pl.BlockSpec((3, tf), lambda i, j: (0, j)),
                  pl.BlockSpec((1, tf), lambda i, j: (0, j))],
        out_specs=pl.BlockSpec((tm, tf), lambda i, j: (i, j)),
        scratch_shapes=[pltpu.VMEM((tm + 2 * CONV_HALO, d), BF16)],
        compiler_params=_params(("parallel", "arbitrary"), 48),
        name="conv_glu",
    )(h, h, h, w_gate, w_val, conv_w, conv_b.reshape(1, f))


def _rope_tables(seq):
    half = HEAD_DIM // 2
    inv = jnp.power(ROPE_BASE, -jnp.arange(0, half, 2, dtype=F32) / half)
    t = jnp.arange(seq)

    def cs(pos):
        ang = pos.astype(F32)[:, None] * inv[None, :]
        return jnp.cos(ang), jnp.sin(ang)

    cr, sr = cs(t // GRID_W)
    cc, sc = cs(t % GRID_W)
    cos_t = jnp.concatenate([cr, cr, cc, cc], axis=1)
    sin_t = jnp.concatenate([-sr, sr, -sc, sc], axis=1)
    return cos_t, sin_t


def _dims(d):
    bw = d // N_BRANCH
    heads = bw // HEAD_DIM
    kv = max(1, heads // 4)
    off_ak = heads * HEAD_DIM
    off_av = off_ak + kv * HEAD_DIM
    off_nq = off_av + kv * HEAD_DIM
    off_nk = off_nq + bw
    off_nv = off_nk + bw
    off_pu = off_nv + bw
    off_fu = off_pu + bw
    off_gt = off_fu + bw
    return dict(bw=bw, a_heads=heads, a_kv=kv, na_heads=heads, off_ak=off_ak, off_av=off_av, off_nq=off_nq,
                off_nk=off_nk, off_nv=off_nv, off_pu=off_pu, off_fu=off_fu, off_gt=off_gt)


def kernel(x, c, ctx, c_ctx, w_mod, b_mod, g_mix, w_in, a_sink, na_rpb, w_pool, pool_scale, w_fnet, w_branch, w_out, g_ffn, w_ff_gate, w_ff_val, ff_conv_w, ff_conv_b, w_ff_down, g_final):
    batch, seq, d = x.shape
    ctx_len = ctx.shape[1]
    depth = w_mod.shape[0]
    dims = _dims(d)
    ctx_row = batch
    assert batch + 1 <= 8

    cond_t = jnp.zeros((d, 8), F32).at[:, :batch].set(c.T).at[:, ctx_row].set(c_ctx)
    mod = _modulation(cond_t, w_mod, b_mod, batch + 1)
    cos_t, sin_t = _rope_tables(seq)

    tm_lat = min(1024, seq)
    tm_ctx = min(256, ctx_len)
    lat_rows = lambda tm: (lambda i: i // (seq // tm))
    ctx_rows = lambda tm: (lambda i: ctx_row)

    xs = x.reshape(batch * seq, d)
    xc = ctx.reshape(batch * ctx_len, d)
    SH1, SC1, GT1, SH2, SC2, GT2 = range(6)

    for l in range(depth):
        last = l == depth - 1
        modv = mod[l].reshape(8 * 6, 1, d)
        w_qkv = w_in[l][:, :dims["off_gt"]].astype(BF16)
        w_gate = w_in[l][:, dims["off_gt"]:].astype(BF16)
        w_br = w_branch[l].astype(BF16)
        w_o = w_out[l].astype(BF16)
        w_pl = w_pool[l].astype(BF16)
        w_fn = w_fnet[l].astype(BF16)
        w_fg = w_ff_gate[l].astype(BF16)
        w_fv = w_ff_val[l].astype(BF16)
        w_fd = w_ff_down[l].astype(BF16)

        def mixer_inputs(xin, rows, tm, rope, slen):
            tmn = min(tm, 256)
            h = _norm(xin, g_mix[l], modv, rows(tmn), SC1, SH1, tmn, BF16)
            return (h,) + tuple(_inproj(h, w_qkv, cos_t, sin_t, dims, tm, rope, slen))

        def finish(xin, h, ys, rows, tm, slen):
            merged = _merge(h, ys, w_gate, w_br, min(tm, 512))
            x1 = _resproj(merged, w_o, xin, modv, rows(tm), GT1, tm, 512, 48)
            tmn = min(tm, 256)
            h2 = _norm(x1, g_ffn[l], modv, rows(tmn), SC2, SH2, tmn, BF16)
            gl = _ffn(h2, w_fg, w_fv, ff_conv_w[l], ff_conv_b[l], tm, slen)
            tmd = min(tm, 512)
            return _resproj(gl, w_fd, x1, modv, rows(tmd), GT2, tmd, 256, 52)

        hc, cqkv, cpu, cfu = mixer_inputs(xc, ctx_rows, tm_ctx, False, ctx_len)
        h, qkv, pu, fu = mixer_inputs(xs, lat_rows, tm_lat, True, seq)

        ya = _win_attn(qkv, cqkv, a_sink[l], dims, batch, seq, ctx_len)
        yn = _nbr_attn(qkv, cqkv, na_rpb[l], dims, batch, seq, ctx_len)
        yp = _pool(pu, w_pl, pool_scale[l], batch, seq)
        yf = _fnet(fu, w_fn, batch, seq)
        xs = finish(xs, h, (ya, yn, yp, yf), lat_rows, tm_lat, seq)

        if not last:
            cya = _ctx_attn(cqkv, a_sink[l], batch, ctx_len, dims["a_heads"], dims["a_kv"],
                            0, dims["off_ak"], dims["off_av"])
            cyn = _ctx_attn(cqkv, None, batch, ctx_len, dims["na_heads"], dims["na_heads"],
                            dims["off_nq"], dims["off_nk"], dims["off_nv"])
            cyp = _pool(cpu, w_pl, pool_scale[l], batch, ctx_len)
            cyf = _fnet(cfu, w_fn, batch, ctx_len)
            xc = finish(xc, hc, (cya, cyn, cyp, cyf), ctx_rows, tm_ctx, ctx_len)

    out = _norm(xs, g_final, None, None, None, None, 256, F32)
    return out.reshape(batch, seq, d)
```
